```python
import jax, jax.numpy as jnp
from jax import lax
import numpy as np

D_MODEL = 2048
BATCH = 4
SEQ = 4096
DEPTH = 1

CHUNK = 64
D_MIX = D_MODEL
SB_WIDTH = D_MIX // 2
LRU_WIDTH = D_MIX - SB_WIDTH
SB_HEAD_DIM = 128
SB_HEADS = SB_WIDTH // SB_HEAD_DIM
LRU_BLOCKS = 8
LRU_BLOCK_DIM = LRU_WIDTH // LRU_BLOCKS
CONV_WIDTH = 4
LRU_C = 8.0
Q_BLOCK = 128
EPS = 1e-6
D_IN = 4 * SB_WIDTH + 2 * LRU_WIDTH

kernel_name = "hybrid_stickbreaking_rglru_block"


def rmsnorm(x, gain):
    xf = x.astype(jnp.float32)
    y = xf * lax.rsqrt(jnp.mean(xf * xf, axis=-1, keepdims=True) + EPS)
    return (y * gain.astype(jnp.float32)).astype(x.dtype)


def stick_breaking_attention(q, k, v):
    B, H, S, Dh = q.shape
    n_blk = S // Q_BLOCK
    q_blocks = q.reshape(B, H, n_blk, Q_BLOCK, Dh).transpose(2, 0, 1, 3, 4)
    k_pos = jnp.arange(S, dtype=jnp.int32)
    scale = Dh ** -0.5

    def one_block(args):
        q_blk, blk = args
        z = jnp.einsum('bhqd,bhkd->bhqk', q_blk, k).astype(jnp.float32) * scale
        q_pos = blk * Q_BLOCK + jnp.arange(Q_BLOCK, dtype=jnp.int32)
        mask = k_pos[None, :] < q_pos[:, None]
        log_fail = jnp.where(mask, jax.nn.log_sigmoid(-z), 0.0)
        after = lax.cumsum(log_fail, axis=3, reverse=True) - log_fail
        log_w = jax.nn.log_sigmoid(z) + after
        w = jnp.where(mask, jnp.exp(log_w), 0.0)
        return jnp.einsum('bhqk,bhkd->bhqd', w.astype(v.dtype), v)

    out = lax.map(one_block, (q_blocks, jnp.arange(n_blk, dtype=jnp.int32)))
    return out.transpose(1, 2, 0, 3, 4).reshape(B, H, S, Dh)


def causal_depthwise_conv(x, w, b):
    S = x.shape[1]
    xp = jnp.pad(x, ((0, 0), (CONV_WIDTH - 1, 0), (0, 0)))
    out = xp[:, 0:S, :] * w[0]
    for i in range(1, CONV_WIDTH):
        out = out + xp[:, i:i + S, :] * w[i]
    return out + b


def rg_lru(x, w_a, b_a, w_x, b_x, lam):
    B, S, C = x.shape
    xb = x.reshape(B, S, LRU_BLOCKS, LRU_BLOCK_DIM)
    r = jax.nn.sigmoid(jnp.einsum('bsgi,gij->bsgj', xb, w_a).reshape(B, S, C) + b_a)
    i = jax.nn.sigmoid(jnp.einsum('bsgi,gij->bsgj', xb, w_x).reshape(B, S, C) + b_x)
    log_a = -LRU_C * r.astype(jnp.float32) * jax.nn.softplus(-lam.astype(jnp.float32))
    a = jnp.exp(log_a)
    u = jnp.sqrt(-jnp.expm1(2.0 * log_a)) * (i * x).astype(jnp.float32)

    def combine(left, right):
        a_l, b_l = left
        a_r, b_r = right
        return a_l * a_r, a_r * b_l + b_r

    _, h = lax.associative_scan(combine, (a, u), axis=1)
    return h.astype(x.dtype)


def setup_inputs(seed: int = 0) -> dict:
    key = jax.random.key(seed)
    ks = jax.random.split(key, 14)
    L = DEPTH
    x = jax.random.normal(ks[0], (BATCH, SEQ, D_MODEL), jnp.float32)
    norm_gain = 1.0 + 0.02 * jax.random.normal(ks[1], (L, D_MODEL), jnp.float32)
    w_in = jax.random.normal(ks[2], (L, D_MODEL, D_IN), jnp.float32) * D_MODEL ** -0.5
    q_norm_gain = 1.0 + 0.02 * jax.random.normal(ks[3], (L, SB_HEAD_DIM), jnp.float32)
    k_norm_gain = 1.0 + 0.02 * jax.random.normal(ks[4], (L, SB_HEAD_DIM), jnp.float32)
    conv_w = jax.random.normal(ks[5], (L, CONV_WIDTH, LRU_WIDTH), jnp.float32) * CONV_WIDTH ** -0.5
    conv_b = 0.01 * jax.random.normal(ks[6], (L, LRU_WIDTH), jnp.float32)
    lru_w_a = jax.random.normal(ks[7], (L, LRU_BLOCKS, LRU_BLOCK_DIM, LRU_BLOCK_DIM), jnp.float32) * LRU_BLOCK_DIM ** -0.5
    lru_b_a = 0.01 * jax.random.normal(ks[8], (L, LRU_WIDTH), jnp.float32)
    lru_w_x = jax.random.normal(ks[9], (L, LRU_BLOCKS, LRU_BLOCK_DIM, LRU_BLOCK_DIM), jnp.float32) * LRU_BLOCK_DIM ** -0.5
    lru_b_x = 0.01 * jax.random.normal(ks[10], (L, LRU_WIDTH), jnp.float32)
    u = jax.random.uniform(ks[11], (L, LRU_WIDTH), jnp.float32, minval=0.9, maxval=0.999)
    s = u ** (1.0 / LRU_C)
    lru_lambda = jnp.log(s) - jnp.log1p(-s)
    w_out = jax.random.normal(ks[12], (L, D_MIX, D_MODEL), jnp.float32) * D_MIX ** -0.5
    return {"x": x, "norm_gain": norm_gain, "w_in": w_in, "q_norm_gain": q_norm_gain,
            "k_norm_gain": k_norm_gain, "conv_w": conv_w, "conv_b": conv_b,
            "lru_w_a": lru_w_a, "lru_b_a": lru_b_a, "lru_w_x": lru_w_x, "lru_b_x": lru_b_x,
            "lru_lambda": lru_lambda, "w_out": w_out}


def reference(x, norm_gain, w_in, q_norm_gain, k_norm_gain, conv_w, conv_b,
              lru_w_a, lru_b_a, lru_w_x, lru_b_x, lru_lambda, w_out):
    B, S, _ = x.shape
    h = x
    for l in range(DEPTH):
        xn = rmsnorm(h, norm_gain[l])
        proj = jnp.einsum('bsd,de->bse', xn, w_in[l])
        q, k, v, g_sb, x_lru, g_lru = jnp.split(
            proj, np.cumsum([SB_WIDTH] * 4 + [LRU_WIDTH]).tolist(), axis=-1)

        def heads(t):
            return t.reshape(B, S, SB_HEADS, SB_HEAD_DIM)
        qh = rmsnorm(heads(q), q_norm_gain[l]).transpose(0, 2, 1, 3)
        kh = rmsnorm(heads(k), k_norm_gain[l]).transpose(0, 2, 1, 3)
        vh = heads(v).transpose(0, 2, 1, 3)
        y_sb = stick_breaking_attention(qh, kh, vh).transpose(0, 2, 1, 3).reshape(B, S, SB_WIDTH)
        y_sb = y_sb * jax.nn.silu(g_sb)

        xc = causal_depthwise_conv(x_lru, conv_w[l], conv_b[l])
        y_lru = rg_lru(xc, lru_w_a[l], lru_b_a[l], lru_w_x[l], lru_b_x[l], lru_lambda[l])
        y_lru = y_lru * jax.nn.silu(g_lru)

        y = jnp.concatenate([y_sb, y_lru], axis=-1)
        h = h + jnp.einsum('bse,ed->bsd', y, w_out[l])
    return h
```

```python
import functools
import math

import jax
import jax.numpy as jnp
from jax import lax
from jax.experimental import pallas as pl
from jax.experimental.pallas import tpu as pltpu

EPS = 1e-6
HEAD_DIM = 128
CONV_WIDTH = 4
LRU_C = 8.0
SUBLANES = 8

IN_TM = 1024
ATT_TQ = 512
ATT_TK = 256
LRU_TS = 1024
OUT_TM = 512

_MiB = 1024 * 1024


def _params(sem, vmem_mib):
    return pltpu.CompilerParams(dimension_semantics=sem, vmem_limit_bytes=vmem_mib * _MiB)


def _head_rmsnorm(res, gain, scale):
    outs = []
    for h in range(res.shape[1] // HEAD_DIM):
        r = res[:, h * HEAD_DIM:(h + 1) * HEAD_DIM]
        ms = jnp.mean(r * r, axis=-1, keepdims=True)
        outs.append(r * lax.rsqrt(ms + EPS) * (gain * scale))
    return jnp.concatenate(outs, axis=1)


def _inproj_kernel(x_ref, ng_ref, w_ref, qg_ref, kg_ref, o_ref, xn_ref, *, q_scale):
    j = pl.program_id(1)

    @pl.when(j == 0)
    def _():
        x = x_ref[...]
        ms = jnp.mean(x * x, axis=-1, keepdims=True)
        xn_ref[...] = (x * lax.rsqrt(ms + EPS) * ng_ref[...]).astype(xn_ref.dtype)

    res = jnp.dot(xn_ref[...], w_ref[...], preferred_element_type=jnp.float32)

    @pl.when(j == 0)
    def _():
        o_ref[0] = _head_rmsnorm(res, qg_ref[...], q_scale).astype(o_ref.dtype)

    @pl.when(j == 1)
    def _():
        o_ref[0] = _head_rmsnorm(res, kg_ref[...], 1.0).astype(o_ref.dtype)

    @pl.when((j == 2) | (j == 4))
    def _():
        o_ref[0] = res.astype(o_ref.dtype)

    @pl.when((j == 3) | (j == 5))
    def _():
        o_ref[0] = (res * jax.nn.sigmoid(res)).astype(o_ref.dtype)


def _inproj(x2, norm_gain, w_in_bf, q_gain, k_gain, width):
    m, d = x2.shape
    n_planes = w_in_bf.shape[1] // width
    q_scale = HEAD_DIM ** -0.5 * math.log2(math.e)
    return pl.pallas_call(
        functools.partial(_inproj_kernel, q_scale=q_scale),
        grid=(m // IN_TM, n_planes),
        in_specs=[
            pl.BlockSpec((IN_TM, d), lambda i, j: (i, 0)),
            pl.BlockSpec((1, d), lambda i, j: (0, 0)),
            pl.BlockSpec((d, width), lambda i, j: (0, j)),
            pl.BlockSpec((1, HEAD_DIM), lambda i, j: (0, 0)),
            pl.BlockSpec((1, HEAD_DIM), lambda i, j: (0, 0)),
        ],
        out_specs=pl.BlockSpec((1, IN_TM, width), lambda i, j: (j, i, 0)),
        out_shape=jax.ShapeDtypeStruct((n_planes, m, width), jnp.bfloat16),
        scratch_shapes=[pltpu.VMEM((IN_TM, d), jnp.bfloat16)],
        compiler_params=_params(("parallel", "arbitrary"), 56),
        name="inproj",
    )(x2, norm_gain, w_in_bf, q_gain, k_gain)


def _attn_kernel(q_ref, k_ref, v_ref, g_ref, tri_ref, o_ref, acc_ref, carry_ref):
    i = pl.program_id(2)
    q = q_ref[0]
    acc_ref[...] = jnp.zeros_like(acc_ref)
    carry_ref[...] = jnp.zeros_like(carry_ref)
    blocks_per_q = ATT_TQ // ATT_TK

    def step(kb, masked):
        start = pl.multiple_of(kb * ATT_TK, ATT_TK)
        k_blk = k_ref[0, pl.ds(start, ATT_TK), :]
        v_blk = v_ref[0, pl.ds(start, ATT_TK), :]
        z = lax.dot_general(q, k_blk, (((1,), (1,)), ((), ())),
                            preferred_element_type=jnp.float32)
        sp = jnp.log2(1.0 + jnp.exp2(-jnp.abs(z)))
        ls = jnp.minimum(z, 0.0) - sp
        lf = ls - z
        if masked:
            q_pos = i * ATT_TQ + lax.broadcasted_iota(jnp.int32, z.shape, 0)
            k_pos = kb * ATT_TK + lax.broadcasted_iota(jnp.int32, z.shape, 1)
            mask = k_pos < q_pos
            lf = jnp.where(mask, lf, 0.0)
        hi = lf.astype(jnp.bfloat16)
        lo = (lf - hi.astype(jnp.float32)).astype(jnp.bfloat16)
        tri = tri_ref[...]
        after = (jnp.dot(hi, tri, preferred_element_type=jnp.float32)
                 + jnp.dot(lo, tri, preferred_element_type=jnp.float32))
        c = carry_ref[...]
        w = jnp.exp2(ls + after + jnp.concatenate([c] * (ATT_TK // HEAD_DIM), axis=1))
        if masked:
            w = jnp.where(mask, w, 0.0)
        acc_ref[...] += jnp.dot(w.astype(jnp.bfloat16), v_blk, preferred_element_type=jnp.float32)
        tot = after[:, 0:1] + lf[:, 0:1]
        carry_ref[...] = c + jnp.broadcast_to(tot, c.shape)

    for d in range(blocks_per_q - 1, -1, -1):
        step(i * blocks_per_q + d, True)

    n_full = i * blocks_per_q

    def body(it, _):
        step(n_full - 1 - it, False)
        return 0

    lax.fori_loop(0, n_full, body, 0)
    o_ref[...] = (acc_ref[...] * g_ref[0].astype(jnp.float32)).astype(o_ref.dtype)


def _attention(planes, tri, batch, seq, heads):
    m = planes.shape[1]
    nq = seq // ATT_TQ
    width = heads * HEAD_DIM
    return pl.pallas_call(
        _attn_kernel,
        grid=(batch, heads, nq),
        in_specs=[
            pl.BlockSpec((1, ATT_TQ, HEAD_DIM), lambda b, h, i: (0, b * nq + i, h)),
            pl.BlockSpec((1, seq, HEAD_DIM), lambda b, h, i: (1, b, h)),
            pl.BlockSpec((1, seq, HEAD_DIM), lambda b, h, i: (2, b, h)),
            pl.BlockSpec((1, ATT_TQ, HEAD_DIM), lambda b, h, i: (3, b * nq + i, h)),
            pl.BlockSpec((ATT_TK, ATT_TK), lambda b, h, i: (0, 0)),
        ],
        out_specs=pl.BlockSpec((ATT_TQ, HEAD_DIM), lambda b, h, i: (b * nq + i, h)),
        out_shape=jax.ShapeDtypeStruct((m, width), jnp.bfloat16),
        scratch_shapes=[pltpu.VMEM((ATT_TQ, HEAD_DIM), jnp.float32),
                        pltpu.VMEM((ATT_TQ, HEAD_DIM), jnp.float32)],
        compiler_params=_params(("parallel", "parallel", "arbitrary"), 32),
        name="sb_attention",
    )(planes, planes, planes, planes, tri)


def _lru_kernel(x_ref, g_ref, cw_ref, cb_ref, wa_ref, ba_ref, wx_ref, bx_ref, lam_ref, o_ref,
                xe_ref, a_ref, u_ref, h_ref):
    s = pl.program_id(1)
    ts = LRU_TS
    n_groups = ts // SUBLANES

    @pl.when(s == 0)
    def _():
        xe_ref[0:SUBLANES, :] = jnp.zeros((SUBLANES, xe_ref.shape[1]), jnp.float32)
        h_ref[...] = jnp.zeros_like(h_ref)

    @pl.when(s > 0)
    def _():
        xe_ref[0:SUBLANES, :] = xe_ref[ts:ts + SUBLANES, :]

    xe_ref[SUBLANES:SUBLANES + ts, :] = x_ref[0].astype(jnp.float32)
    xc = cb_ref[...] + cw_ref[CONV_WIDTH - 1:CONV_WIDTH, :] * xe_ref[SUBLANES:SUBLANES + ts, :]
    for t in range(1, CONV_WIDTH):
        xc = xc + cw_ref[CONV_WIDTH - 1 - t:CONV_WIDTH - t, :] * xe_ref[SUBLANES - t:SUBLANES - t + ts, :]

    xcb = xc.astype(jnp.bfloat16)
    decay = LRU_C * jax.nn.softplus(-lam_ref[...])
    for g in range(xc.shape[1] // HEAD_DIM):
        cols = slice(g * HEAD_DIM, (g + 1) * HEAD_DIM)
        xg = xcb[:, cols]
        r = jax.nn.sigmoid(jnp.dot(xg, wa_ref[g], preferred_element_type=jnp.float32) + ba_ref[:, cols])
        gi = jax.nn.sigmoid(jnp.dot(xg, wx_ref[g], preferred_element_type=jnp.float32) + bx_ref[:, cols])
        a = jnp.exp(-decay[:, cols] * r)
        u = jnp.sqrt((1.0 - a) * (1.0 + a)) * (gi * xc[:, cols])
        a_ref[:, :, cols] = a.reshape(n_groups, SUBLANES, HEAD_DIM)
        u_ref[:, :, cols] = u.reshape(n_groups, SUBLANES, HEAD_DIM)

    a3 = a_ref[...]
    u3 = u_ref[...]
    row = lax.broadcasted_iota(jnp.int32, a3.shape, 1)
    for d in (1, 2, 4):
        keep = row >= d
        a_sh = jnp.where(keep, pltpu.roll(a3, d, 1), 1.0)
        u_sh = jnp.where(keep, pltpu.roll(u3, d, 1), 0.0)
        u3 = u3 + a3 * u_sh
        a3 = a3 * a_sh
    a_ref[...] = a3
    u_ref[...] = u3

    def body(n, h_prev):
        hg = u_ref[n] + a_ref[n] * h_prev
        u_ref[n] = hg
        return jnp.broadcast_to(hg[SUBLANES - 1:SUBLANES, :], hg.shape)

    h_ref[...] = lax.fori_loop(0, n_groups, body, h_ref[...], unroll=8)
    hs = u_ref[...].reshape(ts, -1)
    o_ref[0] = (hs * g_ref[0].astype(jnp.float32)).astype(o_ref.dtype)


def _lru(planes, conv_w, conv_b, w_a, b_a, w_x, b_x, lam, batch, seq):
    m, c = planes.shape[1], planes.shape[2]
    ns = seq // LRU_TS
    vec = pl.BlockSpec((1, c), lambda b, s: (0, 0))
    gate_w = pl.BlockSpec(w_a.shape, lambda b, s: (0, 0, 0))
    return pl.pallas_call(
        _lru_kernel,
        grid=(batch, ns),
        in_specs=[
            pl.BlockSpec((1, LRU_TS, c), lambda b, s: (4, b * ns + s, 0)),
            pl.BlockSpec((1, LRU_TS, c), lambda b, s: (5, b * ns + s, 0)),
            pl.BlockSpec((CONV_WIDTH, c), lambda b, s: (0, 0)),
            vec, gate_w, vec, gate_w, vec, vec,
        ],
        out_specs=pl.BlockSpec((1, LRU_TS, c), lambda b, s: (0, b * ns + s, 0)),
        out_shape=jax.ShapeDtypeStruct((1, m, c), jnp.bfloat16),
        scratch_shapes=[pltpu.VMEM((LRU_TS + SUBLANES, c), jnp.float32),
                        pltpu.VMEM((LRU_TS // SUBLANES, SUBLANES, c), jnp.float32),
                        pltpu.VMEM((LRU_TS // SUBLANES, SUBLANES, c), jnp.float32),
                        pltpu.VMEM((SUBLANES, c), jnp.float32)],
        compiler_params=_params(("parallel", "arbitrary"), 48),
        name="rg_lru",
    )(planes, planes, conv_w, conv_b, w_a, b_a, w_x, b_x, lam)[0]


def _outproj_kernel(x_ref, ysb_ref, ylru_ref, w_ref, o_ref):
    half = ysb_ref.shape[1]
    y = jnp.dot(ysb_ref[...], w_ref[0:half, :], preferred_element_type=jnp.float32)
    y = y + jnp.dot(ylru_ref[...], w_ref[half:2 * half, :], preferred_element_type=jnp.float32)
    o_ref[...] = x_ref[...] + y


def _outproj(x2, y_sb, y_lru, w_out_bf):
    m, d = x2.shape
    half = y_sb.shape[1]
    return pl.pallas_call(
        _outproj_kernel,
        grid=(m // OUT_TM,),
        in_specs=[
            pl.BlockSpec((OUT_TM, d), lambda i: (i, 0)),
            pl.BlockSpec((OUT_TM, half), lambda i: (i, 0)),
            pl.BlockSpec((OUT_TM, half), lambda i: (i, 0)),
            pl.BlockSpec(w_out_bf.shape, lambda i: (0, 0)),
        ],
        out_specs=pl.BlockSpec((OUT_TM, d), lambda i: (i, 0)),
        out_shape=jax.ShapeDtypeStruct((m, d), x2.dtype),
        compiler_params=_params(("parallel",), 48),
        name="outproj",
    )(x2, y_sb, y_lru, w_out_bf)


def kernel(x, norm_gain, w_in, q_norm_gain, k_norm_gain, conv_w, conv_b, lru_w_a, lru_b_a,
           lru_w_x, lru_b_x, lru_lambda, w_out):
    batch, seq, d_model = x.shape
    depth = norm_gain.shape[0]
    width = lru_lambda.shape[1]
    heads = width // HEAD_DIM
    assert w_in.shape[2] == 6 * width and w_out.shape[1] == 2 * width
    assert seq % ATT_TQ == 0 and seq % LRU_TS == 0 and (batch * seq) % IN_TM == 0
    tri = (lax.broadcasted_iota(jnp.int32, (ATT_TK, ATT_TK), 0)
           > lax.broadcasted_iota(jnp.int32, (ATT_TK, ATT_TK), 1)).astype(jnp.bfloat16)
    h = x.reshape(batch * seq, d_model)
    for l in range(depth):
        planes = _inproj(h, norm_gain[l:l + 1], w_in[l].astype(jnp.bfloat16),
                         q_norm_gain[l:l + 1], k_norm_gain[l:l + 1], width)
        y_sb = _attention(planes, tri, batch, seq, heads)
        y_lru = _lru(planes, conv_w[l], conv_b[l:l + 1], lru_w_a[l].astype(jnp.bfloat16),
                     lru_b_a[l:l + 1], lru_w_x[l].astype(jnp.bfloat16), lru_b_x[l:l + 1],
                     lru_lambda[l:l + 1], batch, seq)
        h = _outproj(h, y_sb, y_lru, w_out[l].astype(jnp.bfloat16))
    return h.reshape(batch, seq, d_model)
```

```python
import functools
import math

import jax
import jax.numpy as jnp
from jax import lax
from jax.experimental import pallas as pl
from jax.experimental.pallas import tpu as pltpu

EPS = 1e-6
HEAD_DIM = 128
CONV_WIDTH = 4
LRU_C = 8.0
SUBLANES = 8

IN_TM = 1024
ATT_TQ = 1024
ATT_TK = 256
ATT_UNROLL = ATT_TQ // ATT_TK
LRU_TS = 1024
OUT_TM = 512

_MiB = 1024 * 1024


def _params(sem, vmem_mib):
    return pltpu.CompilerParams(dimension_semantics=sem, vmem_limit_bytes=vmem_mib * _MiB)


def _head_rmsnorm(res, gain, scale):
    outs = []
    for h in range(res.shape[1] // HEAD_DIM):
        r = res[:, h * HEAD_DIM:(h + 1) * HEAD_DIM]
        ms = jnp.mean(r * r, axis=-1, keepdims=True)
        outs.append(r * lax.rsqrt(ms + EPS) * (gain * scale))
    return jnp.concatenate(outs, axis=1)


def _inproj_kernel(x_ref, ng_ref, w_ref, qg_ref, kg_ref, o_ref, xn_ref, *, q_scale):
    j = pl.program_id(1)

    @pl.when(j == 0)
    def _():
        x = x_ref[...]
        ms = jnp.mean(x * x, axis=-1, keepdims=True)
        xn_ref[...] = (x * lax.rsqrt(ms + EPS) * ng_ref[...]).astype(xn_ref.dtype)

    res = jnp.dot(xn_ref[...], w_ref[...], preferred_element_type=jnp.float32)

    @pl.when(j == 0)
    def _():
        o_ref[0] = _head_rmsnorm(res, qg_ref[...], q_scale).astype(o_ref.dtype)

    @pl.when(j == 1)
    def _():
        o_ref[0] = _head_rmsnorm(res, kg_ref[...], 1.0).astype(o_ref.dtype)

    @pl.when((j == 2) | (j == 4))
    def _():
        o_ref[0] = res.astype(o_ref.dtype)

    @pl.when((j == 3) | (j == 5))
    def _():
        o_ref[0] = (res * jax.nn.sigmoid(res)).astype(o_ref.dtype)


def _inproj(x2, norm_gain, w_in_bf, q_gain, k_gain, width):
    m, d = x2.shape
    n_planes = w_in_bf.shape[1] // width
    q_scale = HEAD_DIM ** -0.5 * math.log2(math.e)
    return pl.pallas_call(
        functools.partial(_inproj_kernel, q_scale=q_scale),
        grid=(m // IN_TM, n_planes),
        in_specs=[
            pl.BlockSpec((IN_TM, d), lambda i, j: (i, 0)),
            pl.BlockSpec((1, d), lambda i, j: (0, 0)),
            pl.BlockSpec((d, width), lambda i, j: (0, j)),
            pl.BlockSpec((1, HEAD_DIM), lambda i, j: (0, 0)),
            pl.BlockSpec((1, HEAD_DIM), lambda i, j: (0, 0)),
        ],
        out_specs=pl.BlockSpec((1, IN_TM, width), lambda i, j: (j, i, 0)),
        out_shape=jax.ShapeDtypeStruct((n_planes, m, width), jnp.bfloat16),
        scratch_shapes=[pltpu.VMEM((IN_TM, d), jnp.bfloat16)],
        compiler_params=_params(("parallel", "arbitrary"), 56),
        name="inproj",
    )(x2, norm_gain, w_in_bf, q_gain, k_gain)


def _attn_kernel(q_ref, k_ref, v_ref, g_ref, tri_ref, o_ref, acc_ref, carry_ref):
    i = pl.program_id(2)
    acc_ref[...] = jnp.zeros_like(acc_ref)
    carry_ref[...] = jnp.zeros_like(carry_ref)
    blocks_per_q = ATT_TQ // ATT_TK
    lane_reps = ATT_TK // HEAD_DIM

    def key_start(kb):
        return pl.multiple_of(kb * ATT_TK, ATT_TK)

    def scores(kb, r0):
        k_blk = k_ref[0, pl.ds(key_start(kb), ATT_TK), :]
        return lax.dot_general(q_ref[0, r0:, :], k_blk, (((1,), (1,)), ((), ())),
                               preferred_element_type=jnp.float32)

    def log_terms(z, kb, r0, masked):
        neg_abs = lax.bitcast_convert_type(
            lax.bitcast_convert_type(z, jnp.uint32) | jnp.uint32(0x80000000), jnp.float32)
        sp = jnp.log2(1.0 + jnp.exp2(neg_abs))
        ls = jnp.minimum(z, 0.0) - sp
        lf = ls - z
        mask = None
        if masked:
            q_pos = i * ATT_TQ + r0 + lax.broadcasted_iota(jnp.int32, z.shape, 0)
            k_pos = kb * ATT_TK + lax.broadcasted_iota(jnp.int32, z.shape, 1)
            mask = k_pos < q_pos
            lf = jnp.where(mask, lf, 0.0)
        return ls, lf, mask

    def later_sum(lf):
        hi = lf.astype(jnp.bfloat16)
        lo = (lf - hi.astype(jnp.float32)).astype(jnp.bfloat16)
        tri = tri_ref[...]
        return (jnp.dot(hi, tri, preferred_element_type=jnp.float32)
                + jnp.dot(lo, tri, preferred_element_type=jnp.float32))

    def weights(ls, lf, after, mask, r0):
        c = carry_ref[r0:, :]
        w = jnp.exp2(ls + after + jnp.concatenate([c] * lane_reps, axis=1))
        if mask is not None:
            w = jnp.where(mask, w, 0.0)
        tot = after[:, 0:1] + lf[:, 0:1]
        carry_ref[r0:, :] = c + jnp.broadcast_to(tot, c.shape)
        return w.astype(jnp.bfloat16)

    def accumulate(w, kb, r0):
        v_blk = v_ref[0, pl.ds(key_start(kb), ATT_TK), :]
        acc_ref[r0:, :] += jnp.dot(w, v_blk, preferred_element_type=jnp.float32)

    def sweep(tiles, masked):
        zs = [scores(kb, r0) for kb, r0 in tiles]
        terms, afters = [], []
        for z, (kb, r0) in zip(zs, tiles):
            terms.append(log_terms(z, kb, r0, masked))
            afters.append(later_sum(terms[-1][1]))
        for (ls, lf, mask), after, (kb, r0) in zip(terms, afters, tiles):
            accumulate(weights(ls, lf, after, mask, r0), kb, r0)

    sweep([(i * blocks_per_q + d, d * ATT_TK) for d in range(blocks_per_q - 1, -1, -1)], True)

    n_full = i * blocks_per_q

    def body(it, _):
        sweep([(n_full - 1 - ATT_UNROLL * it - u, 0) for u in range(ATT_UNROLL)], False)
        return 0

    lax.fori_loop(0, n_full // ATT_UNROLL, body, 0)
    o_ref[...] = (acc_ref[...] * g_ref[0].astype(jnp.float32)).astype(o_ref.dtype)


def _attention(planes, tri, batch, seq, heads):
    m = planes.shape[1]
    nq = seq // ATT_TQ
    width = heads * HEAD_DIM
    return pl.pallas_call(
        _attn_kernel,
        grid=(batch, heads, nq),
        in_specs=[
            pl.BlockSpec((1, ATT_TQ, HEAD_DIM), lambda b, h, i: (0, b * nq + i, h)),
            pl.BlockSpec((1, seq, HEAD_DIM), lambda b, h, i: (1, b, h)),
            pl.BlockSpec((1, seq, HEAD_DIM), lambda b, h, i: (2, b, h)),
            pl.BlockSpec((1, ATT_TQ, HEAD_DIM), lambda b, h, i: (3, b * nq + i, h)),
            pl.BlockSpec((ATT_TK, ATT_TK), lambda b, h, i: (0, 0)),
        ],
        out_specs=pl.BlockSpec((ATT_TQ, HEAD_DIM), lambda b, h, i: (b * nq + i, h)),
        out_shape=jax.ShapeDtypeStruct((m, width), jnp.bfloat16),
        scratch_shapes=[pltpu.VMEM((ATT_TQ, HEAD_DIM), jnp.float32),
                        pltpu.VMEM((ATT_TQ, HEAD_DIM), jnp.float32)],
        compiler_params=_params(("parallel", "parallel", "arbitrary"), 32),
        name="sb_attention",
    )(planes, planes, planes, planes, tri)


def _lru_kernel(x_ref, g_ref, cw_ref, cb_ref, wa_ref, ba_ref, wx_ref, bx_ref, lam_ref, o_ref,
                xe_ref, a_ref, u_ref, h_ref):
    s = pl.program_id(1)
    ts = LRU_TS
    n_groups = ts // SUBLANES

    @pl.when(s == 0)
    def _():
        xe_ref[0:SUBLANES, :] = jnp.zeros((SUBLANES, xe_ref.shape[1]), jnp.float32)
        h_ref[...] = jnp.zeros_like(h_ref)

    @pl.when(s > 0)
    def _():
        xe_ref[0:SUBLANES, :] = xe_ref[ts:ts + SUBLANES, :]

    xe_ref[SUBLANES:SUBLANES + ts, :] = x_ref[0].astype(jnp.float32)
    xc = cb_ref[...] + cw_ref[CONV_WIDTH - 1:CONV_WIDTH, :] * xe_ref[SUBLANES:SUBLANES + ts, :]
    for t in range(1, CONV_WIDTH):
        xc = xc + cw_ref[CONV_WIDTH - 1 - t:CONV_WIDTH - t, :] * xe_ref[SUBLANES - t:SUBLANES - t + ts, :]

    xcb = xc.astype(jnp.bfloat16)
    decay = LRU_C * jax.nn.softplus(-lam_ref[...])
    for g in range(xc.shape[1] // HEAD_DIM):
        cols = slice(g * HEAD_DIM, (g + 1) * HEAD_DIM)
        xg = xcb[:, cols]
        r = jax.nn.sigmoid(jnp.dot(xg, wa_ref[g], preferred_element_type=jnp.float32) + ba_ref[:, cols])
        gi = jax.nn.sigmoid(jnp.dot(xg, wx_ref[g], preferred_element_type=jnp.float32) + bx_ref[:, cols])
        a = jnp.exp(-decay[:, cols] * r)
        u = jnp.sqrt((1.0 - a) * (1.0 + a)) * (gi * xc[:, cols])
        a_ref[:, :, cols] = a.reshape(n_groups, SUBLANES, HEAD_DIM)
        u_ref[:, :, cols] = u.reshape(n_groups, SUBLANES, HEAD_DIM)

    a3 = a_ref[...]
    u3 = u_ref[...]
    row = lax.broadcasted_iota(jnp.int32, a3.shape, 1)
    for d in (1, 2, 4):
        keep = row >= d
        a_sh = jnp.where(keep, pltpu.roll(a3, d, 1), 1.0)
        u_sh = jnp.where(keep, pltpu.roll(u3, d, 1), 0.0)
        u3 = u3 + a3 * u_sh
        a3 = a3 * a_sh
    a_ref[...] = a3
    u_ref[...] = u3

    def body(n, h_prev):
        hg = u_ref[n] + a_ref[n] * h_prev
        u_ref[n] = hg
        return jnp.broadcast_to(hg[SUBLANES - 1:SUBLANES, :], hg.shape)

    h_ref[...] = lax.fori_loop(0, n_groups, body, h_ref[...], unroll=8)
    hs = u_ref[...].reshape(ts, -1)
    o_ref[0] = (hs * g_ref[0].astype(jnp.float32)).astype(o_ref.dtype)


def _lru(planes, conv_w, conv_b, w_a, b_a, w_x, b_x, lam, batch, seq):
    m, c = planes.shape[1], planes.shape[2]
    ns = seq // LRU_TS
    vec = pl.BlockSpec((1, c), lambda b, s: (0, 0))
    gate_w = pl.BlockSpec(w_a.shape, lambda b, s: (0, 0, 0))
    return pl.pallas_call(
        _lru_kernel,
        grid=(batch, ns),
        in_specs=[
            pl.BlockSpec((1, LRU_TS, c), lambda b, s: (4, b * ns + s, 0)),
            pl.BlockSpec((1, LRU_TS, c), lambda b, s: (5, b * ns + s, 0)),
            pl.BlockSpec((CONV_WIDTH, c), lambda b, s: (0, 0)),
            vec, gate_w, vec, gate_w, vec, vec,
        ],
        out_specs=pl.BlockSpec((1, LRU_TS, c), lambda b, s: (0, b * ns + s, 0)),
        out_shape=jax.ShapeDtypeStruct((1, m, c), jnp.bfloat16),
        scratch_shapes=[pltpu.VMEM((LRU_TS + SUBLANES, c), jnp.float32),
                        pltpu.VMEM((LRU_TS // SUBLANES, SUBLANES, c), jnp.float32),
                        pltpu.VMEM((LRU_TS // SUBLANES, SUBLANES, c), jnp.float32),
                        pltpu.VMEM((SUBLANES, c), jnp.float32)],
        compiler_params=_params(("parallel", "arbitrary"), 48),
        name="rg_lru",
    )(planes, planes, conv_w, conv_b, w_a, b_a, w_x, b_x, lam)[0]


def _outproj_kernel(x_ref, ysb_ref, ylru_ref, w_ref, o_ref):
    half = ysb_ref.shape[1]
    y = jnp.dot(ysb_ref[...], w_ref[0:half, :], preferred_element_type=jnp.float32)
    y = y + jnp.dot(ylru_ref[...], w_ref[half:2 * half, :], preferred_element_type=jnp.float32)
    o_ref[...] = x_ref[...] + y


def _outproj(x2, y_sb, y_lru, w_out_bf):
    m, d = x2.shape
    half = y_sb.shape[1]
    return pl.pallas_call(
        _outproj_kernel,
        grid=(m // OUT_TM,),
        in_specs=[
            pl.BlockSpec((OUT_TM, d), lambda i: (i, 0)),
            pl.BlockSpec((OUT_TM, half), lambda i: (i, 0)),
            pl.BlockSpec((OUT_TM, half), lambda i: (i, 0)),
            pl.BlockSpec(w_out_bf.shape, lambda i: (0, 0)),
        ],
        out_specs=pl.BlockSpec((OUT_TM, d), lambda i: (i, 0)),
        out_shape=jax.ShapeDtypeStruct((m, d), x2.dtype),
        compiler_params=_params(("parallel",), 48),
        name="outproj",
    )(x2, y_sb, y_lru, w_out_bf)


def kernel(x, norm_gain, w_in, q_norm_gain, k_norm_gain, conv_w, conv_b, lru_w_a, lru_b_a,
           lru_w_x, lru_b_x, lru_lambda, w_out):
    batch, seq, d_model = x.shape
    depth = norm_gain.shape[0]
    width = lru_lambda.shape[1]
    heads = width // HEAD_DIM
    assert w_in.shape[2] == 6 * width and w_out.shape[1] == 2 * width
    assert seq % ATT_TQ == 0 and seq % LRU_TS == 0 and (batch * seq) % IN_TM == 0
    tri = (lax.broadcasted_iota(jnp.int32, (ATT_TK, ATT_TK), 0)
           > lax.broadcasted_iota(jnp.int32, (ATT_TK, ATT_TK), 1)).astype(jnp.bfloat16)
    h = x.reshape(batch * seq, d_model)
    for l in range(depth):
        planes = _inproj(h, norm_gain[l:l + 1], w_in[l].astype(jnp.bfloat16),
                         q_norm_gain[l:l + 1], k_norm_gain[l:l + 1], width)
        y_sb = _attention(planes, tri, batch, seq, heads)
        y_lru = _lru(planes, conv_w[l], conv_b[l:l + 1], lru_w_a[l].astype(jnp.bfloat16),
                     lru_b_a[l:l + 1], lru_w_x[l].astype(jnp.bfloat16), lru_b_x[l:l + 1],
                     lru_lambda[l:l + 1], batch, seq)
        h = _outproj(h, y_sb, y_lru, w_out[l].astype(jnp.bfloat16))
    return h.reshape(batch, seq, d_model)
```

```python
import functools
import math

import jax
import jax.numpy as jnp
from jax import lax
from jax.experimental import pallas as pl
from jax.experimental.pallas import tpu as pltpu

EPS = 1e-6
HEAD_DIM = 128
CONV_WIDTH = 4
LRU_C = 8.0
SUBLANES = 8

IN_TM = 1024
ATT_TQ = 1024
ATT_TK = 256
ATT_UNROLL = ATT_TQ // ATT_TK
LRU_TS = 1024
OUT_TM = 512

_MiB = 1024 * 1024


def _params(sem, vmem_mib):
    return pltpu.CompilerParams(dimension_semantics=sem, vmem_limit_bytes=vmem_mib * _MiB)


def _head_rmsnorm(res, gain):
    outs = []
    for h in range(res.shape[1] // HEAD_DIM):
        r = res[:, h * HEAD_DIM:(h + 1) * HEAD_DIM]
        ms = jnp.mean(r * r, axis=-1, keepdims=True)
        outs.append(r * lax.rsqrt(ms + EPS) * gain)
    return jnp.concatenate(outs, axis=1)


def _inproj_kernel(x_ref, ng_ref, w_ref, qg_ref, kg_ref, o_ref, xn_ref, *, q_scale):
    j = pl.program_id(1)

    @pl.when(j == 0)
    def _():
        x = x_ref[...]
        ms = jnp.mean(x * x, axis=-1, keepdims=True)
        xn_ref[...] = (x * lax.rsqrt(ms + EPS) * ng_ref[...]).astype(xn_ref.dtype)

    def project():
        return jnp.dot(xn_ref[...], w_ref[...], preferred_element_type=jnp.float32)

    @pl.when(j <= 1)
    def _():
        gain = jnp.where(j == 0, qg_ref[...] * q_scale, kg_ref[...])
        o_ref[0] = _head_rmsnorm(project(), gain).astype(o_ref.dtype)

    @pl.when((j == 2) | (j == 4))
    def _():
        o_ref[0] = project().astype(o_ref.dtype)

    @pl.when((j == 3) | (j == 5))
    def _():
        res = project()
        o_ref[0] = (res * jax.nn.sigmoid(res)).astype(o_ref.dtype)


def _inproj(x2, norm_gain, w_in_bf, q_gain, k_gain, width):
    m, d = x2.shape
    n_planes = w_in_bf.shape[1] // width
    q_scale = HEAD_DIM ** -0.5 * math.log2(math.e)
    return pl.pallas_call(
        functools.partial(_inproj_kernel, q_scale=q_scale),
        grid=(m // IN_TM, n_planes),
        in_specs=[
            pl.BlockSpec((IN_TM, d), lambda i, j: (i, 0)),
            pl.BlockSpec((1, d), lambda i, j: (0, 0)),
            pl.BlockSpec((d, width), lambda i, j: (0, j)),
            pl.BlockSpec((1, HEAD_DIM), lambda i, j: (0, 0)),
            pl.BlockSpec((1, HEAD_DIM), lambda i, j: (0, 0)),
        ],
        out_specs=pl.BlockSpec((1, IN_TM, width), lambda i, j: (j, i, 0)),
        out_shape=jax.ShapeDtypeStruct((n_planes, m, width), jnp.bfloat16),
        scratch_shapes=[pltpu.VMEM((IN_TM, d), jnp.bfloat16)],
        compiler_params=_params(("parallel", "arbitrary"), 56),
        name="inproj",
    )(x2, norm_gain, w_in_bf, q_gain, k_gain)


def _mxu_dot(a, b):
    return lax.dot_general(a, b, (((1,), (0,)), ((), ())), preferred_element_type=jnp.float32)


def _attn_kernel(q_ref, k_ref, v_ref, g_ref, tri_ref, o_ref, acc_ref, carry_ref):
    i = pl.program_id(2)
    acc_ref[...] = jnp.zeros_like(acc_ref)
    carry_ref[...] = jnp.zeros_like(carry_ref)
    blocks_per_q = ATT_TQ // ATT_TK
    lane_reps = ATT_TK // HEAD_DIM

    def key_start(kb):
        return pl.multiple_of(kb * ATT_TK, ATT_TK)

    def scores(kb, r0):
        k_blk = k_ref[0, pl.ds(key_start(kb), ATT_TK), :]
        return lax.dot_general(q_ref[0, r0:, :], k_blk, (((1,), (1,)), ((), ())),
                               preferred_element_type=jnp.float32)

    def log_terms(z, kb, r0, masked):
        sp = jnp.log2(1.0 + jnp.exp2(-jnp.abs(z)))
        ls = jnp.minimum(z, 0.0) - sp
        lf = ls - z
        mask = None
        if masked:
            q_pos = i * ATT_TQ + r0 + lax.broadcasted_iota(jnp.int32, z.shape, 0)
            k_pos = kb * ATT_TK + lax.broadcasted_iota(jnp.int32, z.shape, 1)
            mask = k_pos < q_pos
            lf = jnp.where(mask, lf, 0.0)
        return ls, lf, mask

    def later_sum(lf):
        return _mxu_dot(lf, tri_ref[...])

    def weights(ls, lf, after, mask, r0):
        c = carry_ref[r0:, :]
        w = jnp.exp2(ls + after + jnp.concatenate([c] * lane_reps, axis=1))
        if mask is not None:
            w = jnp.where(mask, w, 0.0)
        tot = after[:, 0:1] + lf[:, 0:1]
        carry_ref[r0:, :] = c + jnp.broadcast_to(tot, c.shape)
        return w

    def accumulate(w, kb, r0):
        v_blk = v_ref[0, pl.ds(key_start(kb), ATT_TK), :]
        acc_ref[r0:, :] += _mxu_dot(w, v_blk)

    def sweep(tiles, masked):
        zs = [scores(kb, r0) for kb, r0 in tiles]
        terms, afters = [], []
        for z, (kb, r0) in zip(zs, tiles):
            terms.append(log_terms(z, kb, r0, masked))
            afters.append(later_sum(terms[-1][1]))
        for (ls, lf, mask), after, (kb, r0) in zip(terms, afters, tiles):
            accumulate(weights(ls, lf, after, mask, r0), kb, r0)

    sweep([(i * blocks_per_q + d, d * ATT_TK) for d in range(blocks_per_q - 1, -1, -1)], True)

    n_full = i * blocks_per_q

    def body(it, _):
        sweep([(n_full - 1 - ATT_UNROLL * it - u, 0) for u in range(ATT_UNROLL)], False)
        return 0

    lax.fori_loop(0, n_full // ATT_UNROLL, body, 0)
    o_ref[...] = (acc_ref[...] * g_ref[0].astype(jnp.float32)).astype(o_ref.dtype)


def _attention(planes, tri, batch, seq, heads):
    m = planes.shape[1]
    nq = seq // ATT_TQ
    width = heads * HEAD_DIM
    return pl.pallas_call(
        _attn_kernel,
        grid=(batch, heads, nq),
        in_specs=[
            pl.BlockSpec((1, ATT_TQ, HEAD_DIM), lambda b, h, i: (0, b * nq + i, h)),
            pl.BlockSpec((1, seq, HEAD_DIM), lambda b, h, i: (1, b, h)),
            pl.BlockSpec((1, seq, HEAD_DIM), lambda b, h, i: (2, b, h)),
            pl.BlockSpec((1, ATT_TQ, HEAD_DIM), lambda b, h, i: (3, b * nq + i, h)),
            pl.BlockSpec((ATT_TK, ATT_TK), lambda b, h, i: (0, 0)),
        ],
        out_specs=pl.BlockSpec((ATT_TQ, HEAD_DIM), lambda b, h, i: (b * nq + i, h)),
        out_shape=jax.ShapeDtypeStruct((m, width), jnp.bfloat16),
        scratch_shapes=[pltpu.VMEM((ATT_TQ, HEAD_DIM), jnp.float32),
                        pltpu.VMEM((ATT_TQ, HEAD_DIM), jnp.float32)],
        compiler_params=_params(("parallel", "parallel", "arbitrary"), 32),
        name="sb_attention",
    )(planes, planes, planes, planes, tri)


def _lru_kernel(x_ref, g_ref, cw_ref, cb_ref, wa_ref, ba_ref, wx_ref, bx_ref, lam_ref, o_ref,
                xe_ref, a_ref, u_ref, h_ref):
    s = pl.program_id(1)
    ts = LRU_TS
    n_groups = ts // SUBLANES

    @pl.when(s == 0)
    def _():
        xe_ref[0:SUBLANES, :] = jnp.zeros((SUBLANES, xe_ref.shape[1]), jnp.float32)
        h_ref[...] = jnp.zeros_like(h_ref)

    @pl.when(s > 0)
    def _():
        xe_ref[0:SUBLANES, :] = xe_ref[ts:ts + SUBLANES, :]

    xe_ref[SUBLANES:SUBLANES + ts, :] = x_ref[0].astype(jnp.float32)
    xc = cb_ref[...] + cw_ref[CONV_WIDTH - 1:CONV_WIDTH, :] * xe_ref[SUBLANES:SUBLANES + ts, :]
    for t in range(1, CONV_WIDTH):
        xc = xc + cw_ref[CONV_WIDTH - 1 - t:CONV_WIDTH - t, :] * xe_ref[SUBLANES - t:SUBLANES - t + ts, :]

    xcb = xc.astype(jnp.bfloat16)
    decay = LRU_C * jax.nn.softplus(-lam_ref[...])
    for g in range(xc.shape[1] // HEAD_DIM):
        cols = slice(g * HEAD_DIM, (g + 1) * HEAD_DIM)
        xg = xcb[:, cols]
        r = jax.nn.sigmoid(jnp.dot(xg, wa_ref[g], preferred_element_type=jnp.float32) + ba_ref[:, cols])
        gi = jax.nn.sigmoid(jnp.dot(xg, wx_ref[g], preferred_element_type=jnp.float32) + bx_ref[:, cols])
        a = jnp.exp(-decay[:, cols] * r)
        u = jnp.sqrt((1.0 - a) * (1.0 + a)) * (gi * xc[:, cols])
        a_ref[:, :, cols] = a.reshape(n_groups, SUBLANES, HEAD_DIM)
        u_ref[:, :, cols] = u.reshape(n_groups, SUBLANES, HEAD_DIM)

    a3 = a_ref[...]
    u3 = u_ref[...]
    row = lax.broadcasted_iota(jnp.int32, a3.shape, 1)
    for d in (1, 2, 4):
        keep = row >= d
        a_sh = jnp.where(keep, pltpu.roll(a3, d, 1), 1.0)
        u_sh = jnp.where(keep, pltpu.roll(u3, d, 1), 0.0)
        u3 = u3 + a3 * u_sh
        a3 = a3 * a_sh
    a_ref[...] = a3
    u_ref[...] = u3

    def body(n, h_prev):
        hg = u_ref[n] + a_ref[n] * h_prev
        u_ref[n] = hg
        return jnp.broadcast_to(hg[SUBLANES - 1:SUBLANES, :], hg.shape)

    h_ref[...] = lax.fori_loop(0, n_groups, body, h_ref[...], unroll=8)
    hs = u_ref[...].reshape(ts, -1)
    o_ref[0] = (hs * g_ref[0].astype(jnp.float32)).astype(o_ref.dtype)


def _lru(planes, conv_w, conv_b, w_a, b_a, w_x, b_x, lam, batch, seq):
    m, c = planes.shape[1], planes.shape[2]
    ns = seq // LRU_TS
    vec = pl.BlockSpec((1, c), lambda b, s: (0, 0))
    gate_w = pl.BlockSpec(w_a.shape, lambda b, s: (0, 0, 0))
    return pl.pallas_call(
        _lru_kernel,
        grid=(batch, ns),
        in_specs=[
            pl.BlockSpec((1, LRU_TS, c), lambda b, s: (4, b * ns + s, 0)),
            pl.BlockSpec((1, LRU_TS, c), lambda b, s: (5, b * ns + s, 0)),
            pl.BlockSpec((CONV_WIDTH, c), lambda b, s: (0, 0)),
            vec, gate_w, vec, gate_w, vec, vec,
        ],
        out_specs=pl.BlockSpec((1, LRU_TS, c), lambda b, s: (0, b * ns + s, 0)),
        out_shape=jax.ShapeDtypeStruct((1, m, c), jnp.bfloat16),
        scratch_shapes=[pltpu.VMEM((LRU_TS + SUBLANES, c), jnp.float32),
                        pltpu.VMEM((LRU_TS // SUBLANES, SUBLANES, c), jnp.float32),
                        pltpu.VMEM((LRU_TS // SUBLANES, SUBLANES, c), jnp.float32),
                        pltpu.VMEM((SUBLANES, c), jnp.float32)],
        compiler_params=_params(("parallel", "arbitrary"), 48),
        name="rg_lru",
    )(planes, planes, conv_w, conv_b, w_a, b_a, w_x, b_x, lam)[0]


def _outproj_kernel(x_ref, ysb_ref, ylru_ref, w_ref, o_ref):
    half = ysb_ref.shape[1]
    y = jnp.dot(ysb_ref[...], w_ref[0:half, :], preferred_element_type=jnp.float32)
    y = y + jnp.dot(ylru_ref[...], w_ref[half:2 * half, :], preferred_element_type=jnp.float32)
    o_ref[...] = x_ref[...] + y


def _outproj(x2, y_sb, y_lru, w_out_bf):
    m, d = x2.shape
    half = y_sb.shape[1]
    return pl.pallas_call(
        _outproj_kernel,
        grid=(m // OUT_TM,),
        in_specs=[
            pl.BlockSpec((OUT_TM, d), lambda i: (i, 0)),
            pl.BlockSpec((OUT_TM, half), lambda i: (i, 0)),
            pl.BlockSpec((OUT_TM, half), lambda i: (i, 0)),
            pl.BlockSpec(w_out_bf.shape, lambda i: (0, 0)),
        ],
        out_specs=pl.BlockSpec((OUT_TM, d), lambda i: (i, 0)),
        out_shape=jax.ShapeDtypeStruct((m, d), x2.dtype),
        compiler_params=_params(("parallel",), 48),
        name="outproj",
    )(x2, y_sb, y_lru, w_out_bf)


def kernel(x, norm_gain, w_in, q_norm_gain, k_norm_gain, conv_w, conv_b, lru_w_a, lru_b_a,
           lru_w_x, lru_b_x, lru_lambda, w_out):
    batch, seq, d_model = x.shape
    depth = norm_gain.shape[0]
    width = lru_lambda.shape[1]
    heads = width // HEAD_DIM
    assert w_in.shape[2] == 6 * width and w_out.shape[1] == 2 * width
    assert seq % ATT_TQ == 0 and seq % LRU_TS == 0 and (batch * seq) % IN_TM == 0
    tri = (lax.broadcasted_iota(jnp.int32, (ATT_TK, ATT_TK), 0)
           > lax.broadcasted_iota(jnp.int32, (ATT_TK, ATT_TK), 1)).astype(jnp.bfloat16)
    h = x.reshape(batch * seq, d_model)
    for l in range(depth):
        planes = _inproj(h, norm_gain[l:l + 1], w_in[l].astype(jnp.bfloat16),
                         q_norm_gain[l:l + 1], k_norm_gain[l:l + 1], width)
        y_sb = _attention(planes, tri, batch, seq, heads)
        y_lru = _lru(planes, conv_w[l], conv_b[l:l + 1], lru_w_a[l].astype(jnp.bfloat16),
                     lru_b_a[l:l + 1], lru_w_x[l].astype(jnp.bfloat16), lru_b_x[l:l + 1],
                     lru_lambda[l:l + 1], batch, seq)
        h = _outproj(h, y_sb, y_lru, w_out[l].astype(jnp.bfloat16))
    return h.reshape(batch, seq, d_model)
```

```python
import functools
import math

import jax
import jax.numpy as jnp
from jax import lax
from jax.experimental import pallas as pl
from jax.experimental.pallas import tpu as pltpu

EPS = 1e-6
HEAD_DIM = 128
CONV_WIDTH = 4
LRU_C = 8.0
SUBLANES = 8

IN_TM = 1024
ATT_TQ = 1024
ATT_TK = 256
LRU_TS = 1024
OUT_TM = 512

_MiB = 1024 * 1024


def _params(sem, vmem_mib):
    return pltpu.CompilerParams(dimension_semantics=sem, vmem_limit_bytes=vmem_mib * _MiB)


def _head_rmsnorm(res, gain):
    outs = []
    for h in range(res.shape[1] // HEAD_DIM):
        r = res[:, h * HEAD_DIM:(h + 1) * HEAD_DIM]
        ms = jnp.mean(r * r, axis=-1, keepdims=True)
        outs.append(r * lax.rsqrt(ms + EPS) * gain)
    return jnp.concatenate(outs, axis=1)


def _inproj_kernel(x_ref, ng_ref, w_ref, qg_ref, kg_ref, o_ref, xn_ref, *, q_scale):
    j = pl.program_id(1)

    @pl.when(j == 0)
    def _():
        x = x_ref[...]
        ms = jnp.mean(x * x, axis=-1, keepdims=True)
        xn_ref[...] = (x * lax.rsqrt(ms + EPS) * ng_ref[...]).astype(xn_ref.dtype)

    def project():
        return jnp.dot(xn_ref[...], w_ref[...].astype(xn_ref.dtype), preferred_element_type=jnp.float32)

    @pl.when(j <= 1)
    def _():
        gain = jnp.where(j == 0, qg_ref[...] * q_scale, kg_ref[...])
        o_ref[0] = _head_rmsnorm(project(), gain).astype(o_ref.dtype)

    @pl.when((j == 2) | (j == 4))
    def _():
        o_ref[0] = project().astype(o_ref.dtype)

    @pl.when((j == 3) | (j == 5))
    def _():
        res = project()
        o_ref[0] = (res * jax.nn.sigmoid(res)).astype(o_ref.dtype)


def _inproj(x2, norm_gain, w_in_l, q_gain, k_gain, width):
    m, d = x2.shape
    n_planes = w_in_l.shape[1] // width
    q_scale = HEAD_DIM ** -0.5 * math.log2(math.e)
    return pl.pallas_call(
        functools.partial(_inproj_kernel, q_scale=q_scale),
        grid=(m // IN_TM, n_planes),
        in_specs=[
            pl.BlockSpec((IN_TM, d), lambda i, j: (i, 0)),
            pl.BlockSpec((1, d), lambda i, j: (0, 0)),
            pl.BlockSpec((d, width), lambda i, j: (0, j)),
            pl.BlockSpec((1, HEAD_DIM), lambda i, j: (0, 0)),
            pl.BlockSpec((1, HEAD_DIM), lambda i, j: (0, 0)),
        ],
        out_specs=pl.BlockSpec((1, IN_TM, width), lambda i, j: (j, i, 0)),
        out_shape=jax.ShapeDtypeStruct((n_planes, m, width), jnp.bfloat16),
        scratch_shapes=[pltpu.VMEM((IN_TM, d), jnp.bfloat16)],
        compiler_params=_params(("parallel", "arbitrary"), 56),
        name="inproj",
    )(x2, norm_gain, w_in_l, q_gain, k_gain)


def _mxu_dot(a, b):
    return lax.dot_general(a, b, (((1,), (0,)), ((), ())), preferred_element_type=jnp.float32)


def _attn_kernel(q_ref, k_ref, v_ref, g_ref, tri_ref, o_ref, acc_ref, carry_ref, *, n_q_tiles):
    blocks_per_q = ATT_TQ // ATT_TK
    lane_reps = ATT_TK // HEAD_DIM

    def scores(kb, r0):
        k_blk = k_ref[0, kb * ATT_TK:(kb + 1) * ATT_TK, :]
        return lax.dot_general(q_ref[0, r0:, :], k_blk, (((1,), (1,)), ((), ())),
                               preferred_element_type=jnp.float32)

    def log_terms(z, on_diagonal):
        sp = jnp.log2(1.0 + jnp.exp2(-jnp.abs(z)))
        ls = jnp.minimum(z, 0.0) - sp
        lf = ls - z
        mask = None
        if on_diagonal:
            mask = lax.broadcasted_iota(jnp.int32, z.shape, 1) < lax.broadcasted_iota(jnp.int32, z.shape, 0)
            lf = jnp.where(mask, lf, 0.0)
        after = _mxu_dot(lf, tri_ref[...])
        return ls, lf, mask, after

    def accumulate(ls, lf, mask, after, kb, r0):
        c = carry_ref[r0:, :]
        w = jnp.exp2(ls + after + jnp.concatenate([c] * lane_reps, axis=1))
        if mask is not None:
            w = jnp.where(mask, w, 0.0)
        tot = after[:, 0:1] + lf[:, 0:1]
        carry_ref[r0:, :] = c + jnp.broadcast_to(tot, c.shape)
        acc_ref[r0:, :] += _mxu_dot(w, v_ref[0, kb * ATT_TK:(kb + 1) * ATT_TK, :])

    def sweep(qi):
        tiles = [(qi * blocks_per_q + d, d * ATT_TK, True) for d in range(blocks_per_q - 1, -1, -1)]
        tiles += [(kb, 0, False) for kb in range(qi * blocks_per_q - 1, -1, -1)]
        n = len(tiles)
        acc_ref[...] = jnp.zeros_like(acc_ref)
        carry_ref[...] = jnp.zeros_like(carry_ref)
        zs, terms = {}, {}

        def emit_scores(t):
            if t < n:
                zs[t] = scores(tiles[t][0], tiles[t][1])

        def emit_terms(t):
            if t < n:
                terms[t] = log_terms(zs.pop(t), tiles[t][2])

        emit_scores(0)
        emit_scores(1)
        emit_terms(0)
        for t in range(n):
            emit_scores(t + 2)
            emit_terms(t + 1)
            accumulate(*terms.pop(t), tiles[t][0], tiles[t][1])
        o_ref[...] = (acc_ref[...] * g_ref[0].astype(jnp.float32)).astype(o_ref.dtype)

    i = pl.program_id(2)
    for qi in range(n_q_tiles):
        pl.when(i == qi)(functools.partial(sweep, qi))


def _attention(planes, tri, batch, seq, heads):
    m = planes.shape[1]
    nq = seq // ATT_TQ
    width = heads * HEAD_DIM
    return pl.pallas_call(
        functools.partial(_attn_kernel, n_q_tiles=nq),
        grid=(batch, heads, nq),
        in_specs=[
            pl.BlockSpec((1, ATT_TQ, HEAD_DIM), lambda b, h, i: (0, b * nq + i, h)),
            pl.BlockSpec((1, seq, HEAD_DIM), lambda b, h, i: (1, b, h)),
            pl.BlockSpec((1, seq, HEAD_DIM), lambda b, h, i: (2, b, h)),
            pl.BlockSpec((1, ATT_TQ, HEAD_DIM), lambda b, h, i: (3, b * nq + i, h)),
            pl.BlockSpec((ATT_TK, ATT_TK), lambda b, h, i: (0, 0)),
        ],
        out_specs=pl.BlockSpec((ATT_TQ, HEAD_DIM), lambda b, h, i: (b * nq + i, h)),
        out_shape=jax.ShapeDtypeStruct((m, width), jnp.bfloat16),
        scratch_shapes=[pltpu.VMEM((ATT_TQ, HEAD_DIM), jnp.float32),
                        pltpu.VMEM((ATT_TQ, HEAD_DIM), jnp.float32)],
        compiler_params=_params(("parallel", "parallel", "arbitrary"), 32),
        name="sb_attention",
    )(planes, planes, planes, planes, tri)


def _lru_kernel(x_ref, g_ref, cw_ref, cb_ref, wa_ref, ba_ref, wx_ref, bx_ref, lam_ref, o_ref,
                xe_ref, a_ref, u_ref, h_ref):
    s = pl.program_id(1)
    ts = LRU_TS
    n_groups = ts // SUBLANES

    @pl.when(s == 0)
    def _():
        xe_ref[0:SUBLANES, :] = jnp.zeros((SUBLANES, xe_ref.shape[1]), jnp.float32)
        h_ref[...] = jnp.zeros_like(h_ref)

    @pl.when(s > 0)
    def _():
        xe_ref[0:SUBLANES, :] = xe_ref[ts:ts + SUBLANES, :]

    xe_ref[SUBLANES:SUBLANES + ts, :] = x_ref[0].astype(jnp.float32)
    xc = cb_ref[...] + cw_ref[CONV_WIDTH - 1:CONV_WIDTH, :] * xe_ref[SUBLANES:SUBLANES + ts, :]
    for t in range(1, CONV_WIDTH):
        xc = xc + cw_ref[CONV_WIDTH - 1 - t:CONV_WIDTH - t, :] * xe_ref[SUBLANES - t:SUBLANES - t + ts, :]

    xcb = xc.astype(jnp.bfloat16)
    decay = LRU_C * jax.nn.softplus(-lam_ref[...])
    for g in range(xc.shape[1] // HEAD_DIM):
        cols = slice(g * HEAD_DIM, (g + 1) * HEAD_DIM)
        xg = xcb[:, cols]
        r = jax.nn.sigmoid(jnp.dot(xg, wa_ref[g], preferred_element_type=jnp.float32) + ba_ref[:, cols])
        gi = jax.nn.sigmoid(jnp.dot(xg, wx_ref[g], preferred_element_type=jnp.float32) + bx_ref[:, cols])
        a = jnp.exp(-decay[:, cols] * r)
        u = jnp.sqrt((1.0 - a) * (1.0 + a)) * (gi * xc[:, cols])
        a_ref[:, :, cols] = a.reshape(n_groups, SUBLANES, HEAD_DIM)
        u_ref[:, :, cols] = u.reshape(n_groups, SUBLANES, HEAD_DIM)

    a3 = a_ref[...]
    u3 = u_ref[...]
    row = lax.broadcasted_iota(jnp.int32, a3.shape, 1)
    for d in (1, 2, 4):
        keep = row >= d
        a_sh = jnp.where(keep, pltpu.roll(a3, d, 1), 1.0)
        u_sh = jnp.where(keep, pltpu.roll(u3, d, 1), 0.0)
        u3 = u3 + a3 * u_sh
        a3 = a3 * a_sh
    a_ref[...] = a3
    u_ref[...] = u3

    def body(n, h_prev):
        hg = u_ref[n] + a_ref[n] * h_prev
        u_ref[n] = hg
        return jnp.broadcast_to(hg[SUBLANES - 1:SUBLANES, :], hg.shape)

    h_ref[...] = lax.fori_loop(0, n_groups, body, h_ref[...], unroll=8)
    hs = u_ref[...].reshape(ts, -1)
    o_ref[0] = (hs * g_ref[0].astype(jnp.float32)).astype(o_ref.dtype)


def _lru(planes, conv_w, conv_b, w_a, b_a, w_x, b_x, lam, batch, seq):
    m, c = planes.shape[1], planes.shape[2]
    ns = seq // LRU_TS
    vec = pl.BlockSpec((1, c), lambda b, s: (0, 0))
    gate_w = pl.BlockSpec(w_a.shape, lambda b, s: (0, 0, 0))
    return pl.pallas_call(
        _lru_kernel,
        grid=(batch, ns),
        in_specs=[
            pl.BlockSpec((1, LRU_TS, c), lambda b, s: (4, b * ns + s, 0)),
            pl.BlockSpec((1, LRU_TS, c), lambda b, s: (5, b * ns + s, 0)),
            pl.BlockSpec((CONV_WIDTH, c), lambda b, s: (0, 0)),
            vec, gate_w, vec, gate_w, vec, vec,
        ],
        out_specs=pl.BlockSpec((1, LRU_TS, c), lambda b, s: (0, b * ns + s, 0)),
        out_shape=jax.ShapeDtypeStruct((1, m, c), jnp.bfloat16),
        scratch_shapes=[pltpu.VMEM((LRU_TS + SUBLANES, c), jnp.float32),
                        pltpu.VMEM((LRU_TS // SUBLANES, SUBLANES, c), jnp.float32),
                        pltpu.VMEM((LRU_TS // SUBLANES, SUBLANES, c), jnp.float32),
                        pltpu.VMEM((SUBLANES, c), jnp.float32)],
        compiler_params=_params(("parallel", "arbitrary"), 48),
        name="rg_lru",
    )(planes, planes, conv_w, conv_b, w_a, b_a, w_x, b_x, lam)[0]


def _outproj_kernel(x_ref, ysb_ref, ylru_ref, w_ref, o_ref):
    half = ysb_ref.shape[1]
    y = jnp.dot(ysb_ref[...], w_ref[0:half, :], preferred_element_type=jnp.float32)
    y = y + jnp.dot(ylru_ref[...], w_ref[half:2 * half, :], preferred_element_type=jnp.float32)
    o_ref[...] = x_ref[...] + y


def _outproj(x2, y_sb, y_lru, w_out_bf):
    m, d = x2.shape
    half = y_sb.shape[1]
    return pl.pallas_call(
        _outproj_kernel,
        grid=(m // OUT_TM,),
        in_specs=[
            pl.BlockSpec((OUT_TM, d), lambda i: (i, 0)),
            pl.BlockSpec((OUT_TM, half), lambda i: (i, 0)),
            pl.BlockSpec((OUT_TM, half), lambda i: (i, 0)),
            pl.BlockSpec(w_out_bf.shape, lambda i: (0, 0)),
        ],
        out_specs=pl.BlockSpec((OUT_TM, d), lambda i: (i, 0)),
        out_shape=jax.ShapeDtypeStruct((m, d), x2.dtype),
        compiler_params=_params(("parallel",), 48),
        name="outproj",
    )(x2, y_sb, y_lru, w_out_bf)


def kernel(x, norm_gain, w_in, q_norm_gain, k_norm_gain, conv_w, conv_b, lru_w_a, lru_b_a,
           lru_w_x, lru_b_x, lru_lambda, w_out):
    batch, seq, d_model = x.shape
    depth = norm_gain.shape[0]
    width = lru_lambda.shape[1]
    heads = width // HEAD_DIM
    assert w_in.shape[2] == 6 * width and w_out.shape[1] == 2 * width
    assert seq % ATT_TQ == 0 and seq % LRU_TS == 0 and (batch * seq) % IN_TM == 0
    tri = (lax.broadcasted_iota(jnp.int32, (ATT_TK, ATT_TK), 0)
           > lax.broadcasted_iota(jnp.int32, (ATT_TK, ATT_TK), 1)).astype(jnp.bfloat16)
    h = x.reshape(batch * seq, d_model)
    for l in range(depth):
        planes = _inproj(h, norm_gain[l:l + 1], w_in[l],
                         q_norm_gain[l:l + 1], k_norm_gain[l:l + 1], width)
        y_sb = _attention(planes, tri, batch, seq, heads)
        y_lru = _lru(planes, conv_w[l], conv_b[l:l + 1], lru_w_a[l].astype(jnp.bfloat16),
                     lru_b_a[l:l + 1], lru_w_x[l].astype(jnp.bfloat16), lru_b_x[l:l + 1],
                     lru_lambda[l:l + 1], batch, seq)
        h = _outproj(h, y_sb, y_lru, w_out[l].astype(jnp.bfloat16))
    return h.reshape(batch, seq, d_model)
```
